```python
import math
import jax, jax.numpy as jnp
from jax import lax
import numpy as np

D_MODEL = 1024
BATCH = 8
SEQ = 2048
DEPTH = 2
DEC_BATCH = 128
DEC_SEQ = 4
PAST_LEN = 16384
PAGE_SIZE = 128

CONF_DIM = D_MODEL // 2
CONF_KW = 31
SSM_DINNER = D_MODEL
SSM_HEADDIM = 64
SSM_HEADS = SSM_DINNER // SSM_HEADDIM
SSM_GROUPS = 2
SSM_DSTATE = 128
SSM_KW = 4
SSM_CHUNK = 128
SSM_CONV_DIM = SSM_DINNER + 2 * SSM_GROUPS * SSM_DSTATE
GMLP_DIM = D_MODEL // 2
GMLP_GROUPS = 4
GMLP_CHUNK = 128
D_FF = 2816
FFN_KW = 3
N_BRANCH = 3
EPS = 1e-6
IN_DIM = 2 * CONF_DIM + SSM_DINNER + SSM_CONV_DIM + SSM_HEADS + 2 * GMLP_DIM

kernel_name = "hybrid_conformer_ssd_gmlp_decoder_step"


def rmsnorm(x, g):
    xf = x.astype(jnp.float32)
    y = xf * lax.rsqrt(jnp.mean(xf * xf, axis=-1, keepdims=True) + EPS)
    return (y * g.astype(jnp.float32)).astype(x.dtype)


def layernorm(x, g, b):
    xf = x.astype(jnp.float32)
    mu = jnp.mean(xf, axis=-1, keepdims=True)
    var = jnp.mean(jnp.square(xf - mu), axis=-1, keepdims=True)
    y = (xf - mu) * lax.rsqrt(var + EPS)
    return (y * g.astype(jnp.float32) + b.astype(jnp.float32)).astype(x.dtype)


def causal_dwconv(hist, x, w, b):
    k = w.shape[0]
    xe = jnp.concatenate([hist.astype(x.dtype), x], axis=1)
    y = lax.conv_general_dilated(xe, w[:, None, :].astype(x.dtype), window_strides=(1,),
                                 padding='VALID', dimension_numbers=('NWC', 'WIO', 'NWC'),
                                 feature_group_count=x.shape[-1])
    return y + b.astype(x.dtype), xe[:, xe.shape[1] - (k - 1):]


def ssd(x, dt, a_neg, bm, cm, h0):
    f32 = jnp.float32
    n, t, h, p = x.shape
    g, s = bm.shape[2], bm.shape[3]
    r = h // g
    lc = min(SSM_CHUNK, t)
    tp = -(-t // lc) * lc
    pad = tp - t
    if pad:
        padf = lambda a: jnp.pad(a, [(0, 0), (0, pad)] + [(0, 0)] * (a.ndim - 2))
        x, dt, bm, cm = padf(x), padf(dt), padf(bm), padf(cm)
    nc = tp // lc
    xr = x.astype(f32).reshape(n, nc, lc, g, r, p)
    dtr = dt.astype(f32).reshape(n, nc, lc, g, r)
    br = bm.astype(f32).reshape(n, nc, lc, g, s)
    cr = cm.astype(f32).reshape(n, nc, lc, g, s)
    a_cs = jnp.cumsum(dtr * a_neg.astype(f32).reshape(g, r), axis=2)
    xdt = xr * dtr[..., None]
    a_t = jnp.moveaxis(a_cs, 2, -1)
    causal = jnp.tril(jnp.ones((lc, lc), dtype=bool))
    decay = jnp.exp(jnp.where(causal, a_t[..., :, None] - a_t[..., None, :], -jnp.inf))
    cb = jnp.einsum('bcign,bcjgn->bcgij', cr, br)
    mix = cb[:, :, :, None] * decay
    y_diag = jnp.einsum('bcgrij,bcjgrp->bcigrp', mix, xdt)
    to_end = jnp.exp(a_cs[:, :, -1:] - a_cs)
    chunk_states = jnp.einsum('bclgn,bclgr,bclgrp->bcgrpn', br, to_end, xdt)
    chunk_decay = jnp.exp(a_cs[:, :, -1])

    def step(hc, inp):
        dec, st = inp
        return dec[..., None, None] * hc + st, hc

    h_init = h0.astype(f32).reshape(n, g, r, p, s)
    h_last, h_in = lax.scan(step, h_init, (jnp.moveaxis(chunk_decay, 1, 0), jnp.moveaxis(chunk_states, 1, 0)))
    h_in = jnp.moveaxis(h_in, 0, 1)
    y_off = jnp.einsum('bclgn,bcgrpn,bclgr->bclgrp', cr, h_in, jnp.exp(a_cs))
    y = (y_diag + y_off).reshape(n, tp, h, p)[:, :t]
    return y, h_last.reshape(n, h, p, s)


def chunk_spatial(v, w_s, b_s):
    n, t, dv = v.shape
    cg = dv // GMLP_GROUPS
    tp = -(-t // GMLP_CHUNK) * GMLP_CHUNK
    vp = jnp.pad(v, ((0, 0), (0, tp - t), (0, 0))).reshape(n, tp // GMLP_CHUNK, GMLP_CHUNK, GMLP_GROUPS, cg)
    w = jnp.where(jnp.tril(jnp.ones((GMLP_CHUNK, GMLP_CHUNK), dtype=bool)), w_s, 0)
    sm = jnp.einsum('gts,bksgc->bktgc', w, vp) + b_s.T[:, :, None]
    return sm.reshape(n, tp, dv)[:, :t].astype(v.dtype)


def layer(x, hist_a, h_ssm, hist_ssm, hist_ffn, lp):
    (norm1_g, w_in, w_gate, b_gate, conf_conv_w, conf_conv_b, conf_ln_g, conf_ln_b, conf_w_out,
     ssm_conv_w, ssm_conv_b, ssm_dt_bias, ssm_a_log, ssm_d, ssm_norm_g, ssm_w_out,
     gmlp_ln_g, gmlp_ln_b, gmlp_w_s, gmlp_b_s, gmlp_w_out, w_o,
     norm2_g, ffn_w_up, ffn_conv_w, ffn_conv_b, ffn_w_down) = lp
    n, t, _ = x.shape
    f32 = jnp.float32
    h = rmsnorm(x, norm1_g)
    proj = h @ w_in
    o1 = CONF_DIM
    o2 = o1 + CONF_DIM
    o3 = o2 + SSM_DINNER
    o4 = o3 + SSM_CONV_DIM
    o5 = o4 + SSM_HEADS
    o6 = o5 + GMLP_DIM
    a1, a2, z, xbc, dt_raw, u, v = jnp.split(proj, [o1, o2, o3, o4, o5, o6], axis=-1)
    ga = a1 * jax.nn.sigmoid(a2)
    ca, new_hist_a = causal_dwconv(hist_a, ga, conf_conv_w, conf_conv_b)
    y_a = jax.nn.silu(layernorm(ca, conf_ln_g, conf_ln_b)) @ conf_w_out
    xbc_c, new_hist_ssm = causal_dwconv(hist_ssm, xbc, ssm_conv_w, ssm_conv_b)
    xbc_c = jax.nn.silu(xbc_c)
    xs_, b_, c_ = jnp.split(xbc_c, [SSM_DINNER, SSM_DINNER + SSM_GROUPS * SSM_DSTATE], axis=-1)
    dt = jax.nn.softplus(dt_raw.astype(f32) + ssm_dt_bias.astype(f32))
    a_neg = -jnp.exp(ssm_a_log.astype(f32))
    xh = xs_.reshape(n, t, SSM_HEADS, SSM_HEADDIM)
    y_ssd, new_h = ssd(xh, dt, a_neg,
                       b_.reshape(n, t, SSM_GROUPS, SSM_DSTATE),
                       c_.reshape(n, t, SSM_GROUPS, SSM_DSTATE), h_ssm)
    y_ssd = (y_ssd + ssm_d.astype(f32)[:, None] * xh.astype(f32)).reshape(n, t, SSM_DINNER).astype(x.dtype)
    y_b = rmsnorm(y_ssd * jax.nn.silu(z), ssm_norm_g) @ ssm_w_out
    u = jax.nn.gelu(u)
    vn = layernorm(jax.nn.gelu(v), gmlp_ln_g, gmlp_ln_b)
    y_c = (u * chunk_spatial(vn, gmlp_w_s, gmlp_b_s)) @ gmlp_w_out
    gates = jax.nn.sigmoid(h @ w_gate + b_gate).reshape(n, t, N_BRANCH, D_MODEL)
    mixed = gates[:, :, 0] * y_a + gates[:, :, 1] * y_b + gates[:, :, 2] * y_c
    x = x + mixed @ w_o
    h2 = rmsnorm(x, norm2_g)
    up = h2 @ ffn_w_up
    ag, av = up[..., :D_FF], up[..., D_FF:]
    agc, new_hist_ffn = causal_dwconv(hist_ffn, ag, ffn_conv_w, ffn_conv_b)
    x = x + (jax.nn.silu(agc) * av) @ ffn_w_down
    return x, new_hist_a, new_h, new_hist_ssm, new_hist_ffn, vn


def setup_inputs(seed: int = 0) -> dict:
    key = jax.random.key(seed)
    ks = iter(jax.random.split(key, 48))
    f32 = jnp.float32
    L = DEPTH

    def nrm(shape, scale):
        return jax.random.normal(next(ks), shape, f32) * scale

    def gain(shape):
        return 1.0 + 0.1 * jax.random.normal(next(ks), shape, f32)

    dt0 = jnp.exp(jax.random.uniform(next(ks), (L, SSM_HEADS), f32, math.log(1e-3), math.log(1e-1)))
    dt_bias = dt0 + jnp.log(-jnp.expm1(-dt0))
    a_log = jnp.log(jax.random.uniform(next(ks), (L, SSM_HEADS), f32, 1.0, 16.0))
    return {
        "x_prompt": nrm((BATCH, SEQ, D_MODEL), 1.0),
        "x_sample": nrm((DEC_BATCH, DEC_SEQ, D_MODEL), 1.0),
        "state_conv_a": nrm((L, DEC_BATCH, CONF_KW - 1, CONF_DIM), 0.5),
        "state_ssm": nrm((L, DEC_BATCH, SSM_HEADS, SSM_HEADDIM, SSM_DSTATE), 0.5),
        "state_conv_ssm": nrm((L, DEC_BATCH, SSM_KW - 1, SSM_CONV_DIM), 1.0),
        "state_conv_ffn": nrm((L, DEC_BATCH, FFN_KW - 1, D_FF), 1.0),
        "norm1_g": gain((L, D_MODEL)),
        "w_in": nrm((L, D_MODEL, IN_DIM), D_MODEL ** -0.5),
        "w_gate": nrm((L, D_MODEL, N_BRANCH * D_MODEL), D_MODEL ** -0.5),
        "b_gate": nrm((L, N_BRANCH * D_MODEL), 0.02),
        "conf_conv_w": nrm((L, CONF_KW, CONF_DIM), CONF_KW ** -0.5),
        "conf_conv_b": nrm((L, CONF_DIM), 0.02),
        "conf_ln_g": gain((L, CONF_DIM)),
        "conf_ln_b": nrm((L, CONF_DIM), 0.02),
        "conf_w_out": nrm((L, CONF_DIM, D_MODEL), CONF_DIM ** -0.5),
        "ssm_conv_w": nrm((L, SSM_KW, SSM_CONV_DIM), SSM_KW ** -0.5),
        "ssm_conv_b": nrm((L, SSM_CONV_DIM), 0.02),
        "ssm_dt_bias": dt_bias,
        "ssm_a_log": a_log,
        "ssm_d": gain((L, SSM_HEADS)),
        "ssm_norm_g": gain((L, SSM_DINNER)),
        "ssm_w_out": nrm((L, SSM_DINNER, D_MODEL), SSM_DINNER ** -0.5),
        "gmlp_ln_g": gain((L, GMLP_DIM)),
        "gmlp_ln_b": nrm((L, GMLP_DIM), 0.02),
        "gmlp_w_s": nrm((L, GMLP_GROUPS, GMLP_CHUNK, GMLP_CHUNK), GMLP_CHUNK ** -0.5),
        "gmlp_b_s": gain((L, GMLP_GROUPS, GMLP_CHUNK)),
        "gmlp_w_out": nrm((L, GMLP_DIM, D_MODEL), GMLP_DIM ** -0.5),
        "w_o": nrm((L, D_MODEL, D_MODEL), D_MODEL ** -0.5),
        "norm2_g": gain((L, D_MODEL)),
        "ffn_w_up": nrm((L, D_MODEL, 2 * D_FF), D_MODEL ** -0.5),
        "ffn_conv_w": nrm((L, FFN_KW, D_FF), FFN_KW ** -0.5),
        "ffn_conv_b": nrm((L, D_FF), 0.02),
        "ffn_w_down": nrm((L, D_FF, D_MODEL), D_FF ** -0.5),
        "final_norm_g": gain((D_MODEL,)),
    }


def reference(x_prompt, x_sample, state_conv_a, state_ssm, state_conv_ssm, state_conv_ffn,
              norm1_g, w_in, w_gate, b_gate, conf_conv_w, conf_conv_b, conf_ln_g, conf_ln_b, conf_w_out,
              ssm_conv_w, ssm_conv_b, ssm_dt_bias, ssm_a_log, ssm_d, ssm_norm_g, ssm_w_out,
              gmlp_ln_g, gmlp_ln_b, gmlp_w_s, gmlp_b_s, gmlp_w_out, w_o,
              norm2_g, ffn_w_up, ffn_conv_w, ffn_conv_b, ffn_w_down, final_norm_g):
    xp, xs = x_prompt, x_sample
    nb = xp.shape[0]
    stacked = (norm1_g, w_in, w_gate, b_gate, conf_conv_w, conf_conv_b, conf_ln_g, conf_ln_b, conf_w_out,
               ssm_conv_w, ssm_conv_b, ssm_dt_bias, ssm_a_log, ssm_d, ssm_norm_g, ssm_w_out,
               gmlp_ln_g, gmlp_ln_b, gmlp_w_s, gmlp_b_s, gmlp_w_out, w_o,
               norm2_g, ffn_w_up, ffn_conv_w, ffn_conv_b, ffn_w_down)
    pa, ph, pc, pf = [], [], [], []
    sa, sh, sc, sf, sv = [], [], [], [], []
    for l in range(DEPTH):
        lp = tuple(a[l] for a in stacked)
        xp, na, nh, nc, nf, _ = layer(
            xp,
            jnp.zeros((nb, CONF_KW - 1, CONF_DIM), xp.dtype),
            jnp.zeros((nb, SSM_HEADS, SSM_HEADDIM, SSM_DSTATE), jnp.float32),
            jnp.zeros((nb, SSM_KW - 1, SSM_CONV_DIM), xp.dtype),
            jnp.zeros((nb, FFN_KW - 1, D_FF), xp.dtype),
            lp)
        pa.append(na); ph.append(nh); pc.append(nc); pf.append(nf)
        xs, na, nh, nc, nf, nv = layer(xs, state_conv_a[l], state_ssm[l], state_conv_ssm[l], state_conv_ffn[l], lp)
        sa.append(na); sh.append(nh); sc.append(nc); sf.append(nf); sv.append(nv)
    y_prompt = rmsnorm(xp, final_norm_g)
    y_sample = rmsnorm(xs, final_norm_g)
    new_conv_a_p = jnp.stack(pa)
    new_ssm_p = jnp.stack(ph)
    new_conv_ssm_p = jnp.stack(pc)
    new_conv_ffn_p = jnp.stack(pf)
    new_conv_a_s = jnp.stack(sa)
    new_ssm_s = jnp.stack(sh)
    new_conv_ssm_s = jnp.stack(sc)
    new_conv_ffn_s = jnp.stack(sf)
    new_gmlp_v_s = jnp.stack(sv)
    return (y_prompt, y_sample, new_conv_a_p, new_ssm_p, new_conv_ssm_p, new_conv_ffn_p,
            new_conv_a_s, new_ssm_s, new_conv_ssm_s, new_conv_ffn_s, new_gmlp_v_s)
```

```python
import collections
import functools
import math

import jax
import jax.numpy as jnp
from jax import lax
from jax.experimental import pallas as pl
from jax.experimental.pallas import tpu as pltpu

F32 = jnp.float32
BF16 = jnp.bfloat16
EPS = 1e-6
CHUNK = 128
LANES = 128
SUBLANES = 8
VMEM_LIMIT_BYTES = 56 * 1024 * 1024
PROMPT_TILE = 256
SAMPLE_MIX_BLOCK = 32
SAMPLE_FFN_BLOCK = 64
STATE_BLOCK = 8

Dims = collections.namedtuple(
    "Dims", "d cd kw_a dinner heads headdim groups dstate kw_s conv_dim gd ggroups dff kw_f")


def _sigmoid(x):
    return 1.0 / (1.0 + jnp.exp(-x))


def _silu(x):
    return x * _sigmoid(x)


def _gelu(x):
    return 0.5 * x * (1.0 + jnp.tanh(math.sqrt(2.0 / math.pi) * (x + 0.044715 * (x * x * x))))


def _softplus(x):
    return jnp.maximum(x, 0.0) + jnp.log1p(jnp.exp(-jnp.abs(x)))


def _rmsnorm(x, g):
    return x * lax.rsqrt(jnp.mean(x * x, axis=-1, keepdims=True) + EPS) * g


def _layernorm(x, g, b):
    xc = x - jnp.mean(x, axis=-1, keepdims=True)
    var = jnp.mean(xc * xc, axis=-1, keepdims=True)
    return xc * lax.rsqrt(var + EPS) * g + b


def _mm(a, b):
    return jnp.dot(a, b, preferred_element_type=F32)


def _mm_nt(a, b):
    return lax.dot_general(a, b, (((1,), (1,)), ((), ())), preferred_element_type=F32)


def _mm_tn(a, b):
    return lax.dot_general(a, b, (((0,), (0,)), ((), ())), preferred_element_type=F32)


def _split_bf16(v):
    hi = v.astype(BF16)
    lo = (v - hi.astype(F32)).astype(BF16)
    return jnp.concatenate([hi, lo], axis=1)


def _expand(v, sel2):
    return _mm(_split_bf16(v), sel2)


def _conv_taps(w_ref, b_ref, kw, tap):
    acc = b_ref[...] + w_ref[0:1, :] * tap(0)
    for k in range(1, kw):
        acc = acc + w_ref[k:k + 1, :] * tap(k)
    return acc


def _iota2(shape, dim):
    return lax.broadcasted_iota(jnp.int32, shape, dim)


HIST_A_ROWS = 32
HIST_S_ROWS = 8


def _mixer_kernel(x_ref, g1_ref, win_ref, wgate_ref, bgate_ref,
                  cw_ref, cb_ref, clg_ref, clb_ref, cwo_ref,
                  scw_ref, scb_ref, dtb_ref, alog_ref, dskip_ref, sng_ref, swo_ref,
                  glg_ref, glb_ref, gws_ref, gbst_ref, gwo_ref, wo_ref, e2_ref,
                  xo_ref, hista_ref, hssm_ref, hists_ref,
                  hb_scr, xea_scr, xes_scr, st_scr, y_scr, su_scr, *, dims, tc):
    dm = dims
    t = pl.program_id(1)
    last_t = pl.num_programs(1) - 1
    L = CHUNK
    S = dm.dstate
    pair_w = 2 * dm.headdim
    pairs_per_group = dm.heads // dm.groups // 2
    o_z = 2 * dm.cd
    o_xbc = o_z + dm.dinner
    o_u = o_xbc + dm.conv_dim
    o_v = o_u + dm.gd
    o_dt = o_v + dm.gd

    @pl.when(t == 0)
    def _():
        xea_scr[0:HIST_A_ROWS, :] = jnp.zeros((HIST_A_ROWS, dm.cd), F32)
        xes_scr[0:HIST_S_ROWS, :] = jnp.zeros((HIST_S_ROWS, dm.conv_dim), F32)
        st_scr[...] = jnp.zeros(st_scr.shape, F32)

    x = x_ref[...]
    hb_scr[...] = _rmsnorm(x, g1_ref[...]).astype(BF16)

    a12 = _mm(hb_scr[...], win_ref[:, 0:2 * dm.cd])
    xea_scr[HIST_A_ROWS:HIST_A_ROWS + tc, :] = a12[:, :dm.cd] * _sigmoid(a12[:, dm.cd:])
    base_a = HIST_A_ROWS - (dm.kw_a - 1)
    ca = _conv_taps(cw_ref, cb_ref, dm.kw_a, lambda k: xea_scr[pl.ds(base_a + k, tc), :])
    tail_a = xea_scr[tc:tc + HIST_A_ROWS, :]

    @pl.when(t == last_t)
    def _():
        hista_ref[...] = tail_a

    xea_scr[0:HIST_A_ROWS, :] = tail_a
    ya = _mm(_silu(_layernorm(ca, clg_ref[...], clb_ref[...])).astype(BF16), cwo_ref[...])
    mixed = _sigmoid(_mm(hb_scr[...], wgate_ref[:, 0:dm.d]) + bgate_ref[:, 0:dm.d]) * ya

    xes_scr[HIST_S_ROWS:HIST_S_ROWS + tc, :] = _mm(hb_scr[...], win_ref[:, o_xbc:o_xbc + dm.conv_dim])
    base_s = HIST_S_ROWS - (dm.kw_s - 1)
    xc = _silu(_conv_taps(scw_ref, scb_ref, dm.kw_s, lambda k: xes_scr[pl.ds(base_s + k, tc), :]))
    tail_s = xes_scr[tc:tc + HIST_S_ROWS, :]

    @pl.when(t == last_t)
    def _():
        hists_ref[...] = tail_s

    xes_scr[0:HIST_S_ROWS, :] = tail_s

    dt = _softplus(_mm(hb_scr[...], win_ref[:, o_dt:o_dt + LANES]) + dtb_ref[...])
    a_neg = jnp.where(_iota2((1, LANES), 1) < dm.heads, -jnp.exp(alog_ref[...]), 0.0)
    causal = _iota2((L, L), 0) >= _iota2((L, L), 1)
    tril_f32 = jnp.where(causal, 1.0, 0.0)
    lo = _iota2((1, pair_w), 1) < dm.headdim
    e2 = e2_ref[...]

    for c in range(tc // L):
        r0 = c * L
        dtc = dt[r0:r0 + L]
        acs = jnp.dot(tril_f32, dtc * a_neg, precision=lax.Precision.HIGHEST, preferred_element_type=F32)
        acs_t = acs.T
        dt_t = dtc.T
        last = acs[L - 1:L, :]
        w_all = jnp.exp(last - acs) * dtc
        e_all = jnp.exp(acs)
        decrow = _expand(jnp.broadcast_to(jnp.exp(last), (SUBLANES, LANES)), e2)[0:1]
        for g in range(dm.groups):
            bg = xc[r0:r0 + L, dm.dinner + g * S:dm.dinner + (g + 1) * S]
            cg = xc[r0:r0 + L, dm.dinner + (dm.groups + g) * S:dm.dinner + (dm.groups + g + 1) * S]
            cbm = _mm_nt(cg.astype(BF16), bg.astype(BF16))
            for q in range(g * pairs_per_group, (g + 1) * pairs_per_group):
                cols = slice(q * pair_w, (q + 1) * pair_w)
                h_a, h_b = 2 * q, 2 * q + 1

                def mix(hh):
                    diff = acs[:, hh:hh + 1] - acs_t[hh:hh + 1, :]
                    return jnp.exp(jnp.where(causal, diff, -jnp.inf)) * (cbm * dt_t[hh:hh + 1, :])

                xp = xc[r0:r0 + L, cols]
                stp = st_scr[:, cols]
                lhs = jnp.concatenate([mix(h_a), mix(h_b)], axis=1).astype(BF16)
                rhs = jnp.concatenate([jnp.where(lo, xp, 0.0), jnp.where(lo, 0.0, xp)], axis=0).astype(BF16)
                y = _mm(lhs, rhs)
                lhs2 = jnp.concatenate([cg * e_all[:, h_a:h_a + 1], cg * e_all[:, h_b:h_b + 1]], axis=1)
                rhs2 = jnp.concatenate([jnp.where(lo, stp, 0.0), jnp.where(lo, 0.0, stp)], axis=0)
                y = y + _mm(lhs2.astype(BF16), rhs2.astype(BF16))
                bw = jnp.concatenate([bg * w_all[:, h_a:h_a + 1], bg * w_all[:, h_b:h_b + 1]], axis=1)
                upd = _mm_tn(bw.astype(BF16), xp.astype(BF16))
                st_scr[:, cols] = stp * decrow[:, cols] + jnp.where(lo, upd[:S], upd[S:])
                y_scr[r0:r0 + L, cols] = y + dskip_ref[:, cols] * xp

    @pl.when(t == last_t)
    def _():
        hssm_ref[...] = st_scr[...].T

    z = _mm(hb_scr[...], win_ref[:, o_z:o_z + dm.dinner])
    yb = _mm(_rmsnorm(y_scr[...] * _silu(z), sng_ref[...]).astype(BF16), swo_ref[...])
    mixed = mixed + _sigmoid(_mm(hb_scr[...], wgate_ref[:, dm.d:2 * dm.d]) + bgate_ref[:, dm.d:2 * dm.d]) * yb

    uv = _mm(hb_scr[...], win_ref[:, o_u:o_u + 2 * dm.gd])
    u = _gelu(uv[:, :dm.gd])
    vn = _layernorm(_gelu(uv[:, dm.gd:]), glg_ref[...], glb_ref[...])
    cgw = dm.gd // dm.ggroups
    for g in range(dm.ggroups):
        wt = jnp.where(causal, gws_ref[g], 0.0).astype(BF16)
        for c in range(tc // L):
            r0 = c * L
            s = _mm(wt, vn[r0:r0 + L, g * cgw:(g + 1) * cgw].astype(BF16)) + gbst_ref[:, g:g + 1]
            su_scr[r0:r0 + L, g * cgw:(g + 1) * cgw] = (u[r0:r0 + L, g * cgw:(g + 1) * cgw] * s).astype(BF16)
    yc = _mm(su_scr[...], gwo_ref[...])
    mixed = mixed + _sigmoid(_mm(hb_scr[...], wgate_ref[:, 2 * dm.d:3 * dm.d]) + bgate_ref[:, 2 * dm.d:3 * dm.d]) * yc

    xo_ref[...] = x + _mm(mixed.astype(BF16), wo_ref[...])


def _const_spec(arr):
    nd = arr.ndim
    return pl.BlockSpec(arr.shape, lambda *_: (0,) * nd, pipeline_mode=pl.Buffered(1))


def _prompt_mixer(x, lw, dims, tc):
    nb, seq, d = x.shape
    assert seq % tc == 0 and tc % CHUNK == 0
    dm = dims
    consts = [lw[k] for k in ("g1", "win", "wgate", "bgate", "cw", "cb", "clg", "clb", "cwo",
                              "scw", "scb", "dtb", "alog", "dskip", "sng", "swo",
                              "glg", "glb", "gws", "gbst", "gwo", "wo", "e2")]
    hd = dm.heads * dm.headdim
    out_shape = (jax.ShapeDtypeStruct((nb, seq, d), F32),
                 jax.ShapeDtypeStruct((nb, HIST_A_ROWS, dm.cd), F32),
                 jax.ShapeDtypeStruct((nb, hd, dm.dstate), F32),
                 jax.ShapeDtypeStruct((nb, HIST_S_ROWS, dm.conv_dim), F32))
    return pl.pallas_call(
        functools.partial(_mixer_kernel, dims=dm, tc=tc),
        grid=(nb, seq // tc),
        in_specs=[pl.BlockSpec((None, tc, d), lambda b, t: (b, t, 0))] + [_const_spec(c) for c in consts],
        out_specs=(pl.BlockSpec((None, tc, d), lambda b, t: (b, t, 0)),
                   pl.BlockSpec((None, HIST_A_ROWS, dm.cd), lambda b, t: (b, 0, 0)),
                   pl.BlockSpec((None, hd, dm.dstate), lambda b, t: (b, 0, 0)),
                   pl.BlockSpec((None, HIST_S_ROWS, dm.conv_dim), lambda b, t: (b, 0, 0))),
        out_shape=out_shape,
        scratch_shapes=[pltpu.VMEM((tc, d), BF16),
                        pltpu.VMEM((HIST_A_ROWS + tc, dm.cd), F32),
                        pltpu.VMEM((HIST_S_ROWS + tc, dm.conv_dim), F32),
                        pltpu.VMEM((dm.dstate, hd), F32),
                        pltpu.VMEM((tc, hd), F32),
                        pltpu.VMEM((tc, dm.gd), BF16)],
        compiler_params=pltpu.CompilerParams(dimension_semantics=("arbitrary", "arbitrary"),
                                             vmem_limit_bytes=VMEM_LIMIT_BYTES),
        name="prompt_mixer",
    )(x, *consts)


def _ffn_kernel(x_ref, g2_ref, wup_ref, fcw_ref, fcb_ref, wdown_ref, fng_ref,
                xo_ref, histf_ref, hb_scr, xef_scr, act_scr, *, dims, tc, final_norm, col_blocks):
    dm = dims
    t = pl.program_id(1)
    last_t = pl.num_programs(1) - 1

    @pl.when(t == 0)
    def _():
        xef_scr[0:HIST_S_ROWS, :] = jnp.zeros((HIST_S_ROWS, dm.dff), F32)

    x = x_ref[...]
    hb_scr[...] = _rmsnorm(x, g2_ref[...]).astype(BF16)
    base = HIST_S_ROWS - (dm.kw_f - 1)
    bw = dm.dff // col_blocks
    for j in range(col_blocks):
        cols = slice(j * bw, (j + 1) * bw)
        xef_scr[HIST_S_ROWS:HIST_S_ROWS + tc, cols] = _mm(hb_scr[...], wup_ref[:, j * bw:(j + 1) * bw])
        agc = fcb_ref[:, cols] + fcw_ref[0:1, cols] * xef_scr[pl.ds(base, tc), cols]
        for k in range(1, dm.kw_f):
            agc = agc + fcw_ref[k:k + 1, cols] * xef_scr[pl.ds(base + k, tc), cols]
        av = _mm(hb_scr[...], wup_ref[:, dm.dff + j * bw:dm.dff + (j + 1) * bw])
        act_scr[:, cols] = (_silu(agc) * av).astype(BF16)
    tail = xef_scr[tc:tc + HIST_S_ROWS, :]

    @pl.when(t == last_t)
    def _():
        histf_ref[...] = tail

    xef_scr[0:HIST_S_ROWS, :] = tail
    out = x + _mm(act_scr[...], wdown_ref[...])
    if final_norm:
        out = _rmsnorm(out, fng_ref[...])
    xo_ref[...] = out


def _prompt_ffn(x, lw, dims, tc, final_norm):
    nb, seq, d = x.shape
    dm = dims
    consts = [lw[k] for k in ("g2", "wup", "fcw", "fcb", "wdown", "fng")]
    col_blocks = 2 if (dm.dff // 2) % LANES == 0 else 1
    return pl.pallas_call(
        functools.partial(_ffn_kernel, dims=dm, tc=tc, final_norm=final_norm, col_blocks=col_blocks),
        grid=(nb, seq // tc),
        in_specs=[pl.BlockSpec((None, tc, d), lambda b, t: (b, t, 0))] + [_const_spec(c) for c in consts],
        out_specs=(pl.BlockSpec((None, tc, d), lambda b, t: (b, t, 0)),
                   pl.BlockSpec((None, HIST_S_ROWS, dm.dff), lambda b, t: (b, 0, 0))),
        out_shape=(jax.ShapeDtypeStruct((nb, seq, d), F32),
                   jax.ShapeDtypeStruct((nb, HIST_S_ROWS, dm.dff), F32)),
        scratch_shapes=[pltpu.VMEM((tc, d), BF16),
                        pltpu.VMEM((HIST_S_ROWS + tc, dm.dff), F32),
                        pltpu.VMEM((tc, dm.dff), BF16)],
        compiler_params=pltpu.CompilerParams(dimension_semantics=("arbitrary", "arbitrary"),
                                             vmem_limit_bytes=VMEM_LIMIT_BYTES),
        name="prompt_ffn",
    )(x, *consts)


def _s1_kernel(x_ref, ha_ref, hs_ref, g1_ref, win_ref, wgate_ref, bgate_ref,
               cw_ref, cb_ref, clg_ref, clb_ref, cwo_ref,
               scw_ref, scb_ref, dtb_ref, alog_ref, dskip_ref,
               glg_ref, glb_ref, gwst_ref, gbst_ref, gwo_ref, e2_ref, r2_ref,
               mixed_ref, ypart_ref, eexp_ref, z_ref, gate1_ref, xw_ref, b_ref, c_ref, dec_ref,
               nha_ref, nhs_ref, vn_ref, hb_scr, *, dims, ts, nb):
    dm = dims
    S2 = dm.groups * dm.dstate
    o_z = 2 * dm.cd
    o_xbc = o_z + dm.dinner
    o_u = o_xbc + dm.conv_dim
    o_v = o_u + dm.gd
    o_dt = o_v + dm.gd
    rows = lambda a, i: a[i * nb:(i + 1) * nb]

    for i in range(ts):
        hb_scr[i * nb:(i + 1) * nb, :] = _rmsnorm(x_ref[i], g1_ref[...]).astype(BF16)
    hb = hb_scr[...]

    a12 = _mm(hb, win_ref[:, 0:2 * dm.cd])
    ga = a12[:, :dm.cd] * _sigmoid(a12[:, dm.cd:])
    na = dm.kw_a - 1
    xe_a = lambda j: ha_ref[j] if j < na else rows(ga, j - na)
    ca = jnp.concatenate([_conv_taps(cw_ref, cb_ref, dm.kw_a, lambda k, i=i: xe_a(i + k)) for i in range(ts)], axis=0)
    for j in range(na):
        nha_ref[j] = xe_a(j + ts)
    ya = _mm(_silu(_layernorm(ca, clg_ref[...], clb_ref[...])).astype(BF16), cwo_ref[...])
    mixed = _sigmoid(_mm(hb, wgate_ref[:, 0:dm.d]) + bgate_ref[:, 0:dm.d]) * ya

    xbc = _mm(hb, win_ref[:, o_xbc:o_xbc + dm.conv_dim])
    ns = dm.kw_s - 1
    xe_s = lambda j: hs_ref[j] if j < ns else rows(xbc, j - ns)
    xc = [_silu(_conv_taps(scw_ref, scb_ref, dm.kw_s, lambda k, i=i: xe_s(i + k))) for i in range(ts)]
    for j in range(ns):
        nhs_ref[j] = xe_s(j + ts)
    xs = [v[:, :dm.dinner] for v in xc]
    bm = [v[:, dm.dinner:dm.dinner + S2] for v in xc]
    cm = [v[:, dm.dinner + S2:dm.dinner + 2 * S2] for v in xc]

    z = _mm(hb, win_ref[:, o_z:o_z + dm.dinner])
    gate1 = _sigmoid(_mm(hb, wgate_ref[:, dm.d:2 * dm.d]) + bgate_ref[:, dm.d:2 * dm.d])
    dt_all = _softplus(_mm(hb, win_ref[:, o_dt:o_dt + LANES]) + dtb_ref[...])
    a_neg = jnp.where(_iota2((1, LANES), 1) < dm.heads, -jnp.exp(alog_ref[...]), 0.0)
    dt = [rows(dt_all, i) for i in range(ts)]
    acs = []
    for i in range(ts):
        a_i = dt[i] * a_neg
        acs.append(a_i if i == 0 else acs[-1] + a_i)
    e2 = e2_ref[...]
    r2 = r2_ref[...]
    for i in range(ts):
        yd = dskip_ref[...] * xs[i]
        for j in range(i + 1):
            cb16 = _expand(cm[i] * bm[j], r2)
            coef = cb16 * jnp.exp(acs[i] - acs[j]) * dt[j]
            yd = yd + _expand(coef, e2) * xs[j]
        ypart_ref[i] = yd
        eexp_ref[i] = _expand(jnp.exp(acs[i]), e2)
        xw_ref[i] = _expand(jnp.exp(acs[ts - 1] - acs[i]) * dt[i], e2) * xs[i]
        b_ref[i] = bm[i]
        c_ref[i] = cm[i]
        z_ref[i] = rows(z, i)
        gate1_ref[i] = rows(gate1, i)
    for i in range(ts, xw_ref.shape[0]):
        xw_ref[i] = jnp.zeros((nb, dm.dinner), F32)
        b_ref[i] = jnp.zeros((nb, S2), F32)
        c_ref[i] = jnp.zeros((nb, S2), F32)
    dec_ref[...] = jnp.exp(acs[ts - 1])

    uv = _mm(hb, win_ref[:, o_u:o_u + 2 * dm.gd])
    u = _gelu(uv[:, :dm.gd])
    vn_all = _layernorm(_gelu(uv[:, dm.gd:]), glg_ref[...], glb_ref[...])
    vn = [rows(vn_all, i) for i in range(ts)]
    su = []
    for i in range(ts):
        vn_ref[i] = vn[i]
        s = gbst_ref[i:i + 1, :]
        for j in range(i + 1):
            s = s + gwst_ref[i * ts + j:i * ts + j + 1, :] * vn[j]
        su.append(rows(u, i) * s)
    yc = _mm(jnp.concatenate(su, axis=0).astype(BF16), gwo_ref[...])
    mixed = mixed + _sigmoid(_mm(hb, wgate_ref[:, 2 * dm.d:3 * dm.d]) + bgate_ref[:, 2 * dm.d:3 * dm.d]) * yc
    for i in range(ts):
        mixed_ref[i] = rows(mixed, i)


def _tm_spec(lead, nb, width):
    return pl.BlockSpec((lead, nb, width), lambda i: (0, i, 0))


def _sample_s1(x_tm, ha_tm, hs_tm, lw, dims, nb):
    ts, ns, d = x_tm.shape
    dm = dims
    S2 = dm.groups * dm.dstate
    consts = [lw[k] for k in ("g1", "win", "wgate", "bgate", "cw", "cb", "clg", "clb", "cwo",
                              "scw", "scb", "dtb", "alog", "dskip", "glg", "glb", "gwst", "gbss", "gwo", "e2", "r2")]
    f = lambda lead, w: jax.ShapeDtypeStruct((lead, ns, w), F32)
    out_shape = (f(ts, d), f(ts, dm.dinner), f(ts, dm.dinner), f(ts, dm.dinner), f(ts, d),
                 f(SUBLANES, dm.dinner), f(SUBLANES, S2), f(SUBLANES, S2),
                 jax.ShapeDtypeStruct((ns, LANES), F32),
                 f(dm.kw_a - 1, dm.cd), f(dm.kw_s - 1, dm.conv_dim), f(ts, dm.gd))
    out_specs = (_tm_spec(ts, nb, d), _tm_spec(ts, nb, dm.dinner), _tm_spec(ts, nb, dm.dinner),
                 _tm_spec(ts, nb, dm.dinner), _tm_spec(ts, nb, d),
                 _tm_spec(SUBLANES, nb, dm.dinner), _tm_spec(SUBLANES, nb, S2), _tm_spec(SUBLANES, nb, S2),
                 pl.BlockSpec((nb, LANES), lambda i: (i, 0)),
                 _tm_spec(dm.kw_a - 1, nb, dm.cd), _tm_spec(dm.kw_s - 1, nb, dm.conv_dim), _tm_spec(ts, nb, dm.gd))
    return pl.pallas_call(
        functools.partial(_s1_kernel, dims=dm, ts=ts, nb=nb),
        grid=(ns // nb,),
        in_specs=[_tm_spec(ts, nb, d), _tm_spec(dm.kw_a - 1, nb, dm.cd), _tm_spec(dm.kw_s - 1, nb, dm.conv_dim)]
                 + [_const_spec(c) for c in consts],
        out_specs=out_specs,
        out_shape=out_shape,
        scratch_shapes=[pltpu.VMEM((ts * nb, d), BF16)],
        compiler_params=pltpu.CompilerParams(dimension_semantics=("arbitrary",),
                                             vmem_limit_bytes=VMEM_LIMIT_BYTES),
        name="sample_mixer",
    )(x_tm, ha_tm, hs_tm, *consts)


def _s2_kernel(dec_ref, h0_ref, xw_ref, b_ref, c_ref, hn_ref, yo_ref, *, dims, nb):
    dm = dims
    S = dm.dstate
    gw = dm.dinner // dm.groups
    heads_per_group = dm.heads // dm.groups
    i = pl.program_id(0)
    for j in range(nb):
        for g in range(dm.groups):
            r0 = g * gw
            h0g = h0_ref[j, r0:r0 + gw, :]
            cg = c_ref[j, :, g * S:(g + 1) * S]
            yo_ref[j, :, r0:r0 + gw] = _mm_nt(cg.astype(BF16), h0g.astype(BF16))
            upd = _mm_tn(xw_ref[j, :, r0:r0 + gw].astype(BF16), b_ref[j, :, g * S:(g + 1) * S].astype(BF16))
            for hh in range(heads_per_group):
                rr = slice(r0 + hh * dm.headdim, r0 + (hh + 1) * dm.headdim)
                dec = dec_ref[i * nb + j, g * heads_per_group + hh]
                hn_ref[j, rr, :] = dec * h0_ref[j, rr, :] + upd[hh * dm.headdim:(hh + 1) * dm.headdim]


def _sample_s2(dec, h0, xw_sm, b_sm, c_sm, dims, nb):
    ns, hd, S = h0.shape
    dm = dims
    S2 = dm.groups * dm.dstate
    blk = lambda r, w: pl.BlockSpec((nb, r, w), lambda i: (i, 0, 0))
    return pl.pallas_call(
        functools.partial(_s2_kernel, dims=dm, nb=nb),
        grid=(ns // nb,),
        in_specs=[pl.BlockSpec(memory_space=pltpu.SMEM),
                  blk(hd, S), blk(SUBLANES, hd), blk(SUBLANES, S2), blk(SUBLANES, S2)],
        out_specs=(blk(hd, S), blk(SUBLANES, hd)),
        out_shape=(jax.ShapeDtypeStruct((ns, hd, S), F32), jax.ShapeDtypeStruct((ns, SUBLANES, hd), F32)),
        compiler_params=pltpu.CompilerParams(dimension_semantics=("arbitrary",),
                                             vmem_limit_bytes=VMEM_LIMIT_BYTES),
        name="sample_state",
    )(dec, h0, xw_sm, b_sm, c_sm)


def _s3_kernel(x_ref, mixed_ref, ypart_ref, eexp_ref, yoff_ref, z_ref, gate1_ref, hf_ref,
               sng_ref, swo_ref, wo_ref, g2_ref, wup_ref, fcw_ref, fcb_ref, wdown_ref, fng_ref,
               xo_ref, nhf_ref, *, dims, ts, nb, final_norm):
    dm = dims
    cat = lambda ref: jnp.concatenate([ref[i] for i in range(ts)], axis=0)
    rows = lambda a, i: a[i * nb:(i + 1) * nb]
    y = (cat(ypart_ref) + cat(eexp_ref) * cat(yoff_ref)) * _silu(cat(z_ref))
    yb = _mm(_rmsnorm(y, sng_ref[...]).astype(BF16), swo_ref[...])
    mixed = cat(mixed_ref) + cat(gate1_ref) * yb
    x1 = cat(x_ref) + _mm(mixed.astype(BF16), wo_ref[...])

    hb = _rmsnorm(x1, g2_ref[...]).astype(BF16)
    ag = _mm(hb, wup_ref[:, 0:dm.dff])
    av = _mm(hb, wup_ref[:, dm.dff:2 * dm.dff])
    nf = dm.kw_f - 1
    xe = lambda j: hf_ref[j] if j < nf else rows(ag, j - nf)
    agc = jnp.concatenate([_conv_taps(fcw_ref, fcb_ref, dm.kw_f, lambda k, i=i: xe(i + k)) for i in range(ts)], axis=0)
    for j in range(nf):
        nhf_ref[j] = xe(j + ts)
    out = x1 + _mm((_silu(agc) * av).astype(BF16), wdown_ref[...])
    if final_norm:
        out = _rmsnorm(out, fng_ref[...])
    for i in range(ts):
        xo_ref[i] = rows(out, i)


def _sample_s3(x_tm, mixed, ypart, eexp, yoff_tm, z, gate1, hf_tm, lw, dims, nb, final_norm):
    ts, ns, d = x_tm.shape
    dm = dims
    consts = [lw[k] for k in ("sng", "swo", "wo", "g2", "wup", "fcw", "fcb", "wdown", "fng")]
    return pl.pallas_call(
        functools.partial(_s3_kernel, dims=dm, ts=ts, nb=nb, final_norm=final_norm),
        grid=(ns // nb,),
        in_specs=[_tm_spec(ts, nb, d), _tm_spec(ts, nb, d), _tm_spec(ts, nb, dm.dinner), _tm_spec(ts, nb, dm.dinner),
                  _tm_spec(ts, nb, dm.dinner), _tm_spec(ts, nb, dm.dinner), _tm_spec(ts, nb, d),
                  _tm_spec(dm.kw_f - 1, nb, dm.dff)] + [_const_spec(c) for c in consts],
        out_specs=(_tm_spec(ts, nb, d), _tm_spec(dm.kw_f - 1, nb, dm.dff)),
        out_shape=(jax.ShapeDtypeStruct((ts, ns, d), F32), jax.ShapeDtypeStruct((dm.kw_f - 1, ns, dm.dff), F32)),
        compiler_params=pltpu.CompilerParams(dimension_semantics=("arbitrary",),
                                             vmem_limit_bytes=VMEM_LIMIT_BYTES),
        name="sample_out_ffn",
    )(x_tm, mixed, ypart, eexp, yoff_tm, z, gate1, hf_tm, *consts)


def _pad_lanes(v, width=LANES):
    return jnp.pad(v, [(0, 0)] * (v.ndim - 1) + [(0, width - v.shape[-1])])


def _layer_weights(l, dims, ts, p):
    dm = dims
    row = lambda v: v.reshape(1, -1)
    w_in = p["w_in"][l]
    o_dt = 2 * dm.cd + dm.dinner + dm.conv_dim
    win = jnp.concatenate([w_in[:, :o_dt], w_in[:, o_dt + dm.heads:], _pad_lanes(w_in[:, o_dt:o_dt + dm.heads])],
                          axis=1).astype(BF16)
    hd = dm.heads * dm.headdim
    head_of_col = jnp.arange(hd) // dm.headdim
    e1 = (jnp.arange(LANES)[:, None] == head_of_col[None, :])
    group_of_row = jnp.arange(dm.groups * dm.dstate) // dm.dstate
    group_of_head = jnp.arange(LANES) // (dm.heads // dm.groups)
    r1 = (group_of_row[:, None] == group_of_head[None, :]) & (jnp.arange(LANES)[None, :] < dm.heads)
    cgw = dm.gd // dm.ggroups
    gws = p["gmlp_w_s"][l]
    gbs = p["gmlp_b_s"][l]
    return dict(
        g1=row(p["norm1_g"][l]), win=win, wgate=p["w_gate"][l].astype(BF16), bgate=row(p["b_gate"][l]),
        cw=p["conf_conv_w"][l], cb=row(p["conf_conv_b"][l]), clg=row(p["conf_ln_g"][l]), clb=row(p["conf_ln_b"][l]),
        cwo=p["conf_w_out"][l].astype(BF16),
        scw=p["ssm_conv_w"][l], scb=row(p["ssm_conv_b"][l]),
        dtb=_pad_lanes(row(p["ssm_dt_bias"][l])), alog=_pad_lanes(row(p["ssm_a_log"][l])),
        dskip=row(jnp.repeat(p["ssm_d"][l], dm.headdim)),
        sng=row(p["ssm_norm_g"][l]), swo=p["ssm_w_out"][l].astype(BF16),
        glg=row(p["gmlp_ln_g"][l]), glb=row(p["gmlp_ln_b"][l]),
        gws=gws, gbst=gbs.T, gwo=p["gmlp_w_out"][l].astype(BF16), wo=p["w_o"][l].astype(BF16),
        gwst=jnp.repeat(jnp.transpose(gws[:, :ts, :ts], (1, 2, 0)).reshape(ts * ts, dm.ggroups), cgw, axis=1),
        gbss=jnp.repeat(gbs[:, :ts].T, cgw, axis=1),
        e2=jnp.concatenate([e1, e1], axis=0).astype(BF16), r2=jnp.concatenate([r1, r1], axis=0).astype(BF16),
        g2=row(p["norm2_g"][l]), wup=p["ffn_w_up"][l].astype(BF16), fcw=p["ffn_conv_w"][l],
        fcb=row(p["ffn_conv_b"][l]), wdown=p["ffn_w_down"][l].astype(BF16), fng=row(p["final_norm_g"]),
    )


def kernel(x_prompt, x_sample, state_conv_a, state_ssm, state_conv_ssm, state_conv_ffn, norm1_g, w_in, w_gate, b_gate, conf_conv_w, conf_conv_b, conf_ln_g, conf_ln_b, conf_w_out, ssm_conv_w, ssm_conv_b, ssm_dt_bias, ssm_a_log, ssm_d, ssm_norm_g, ssm_w_out, gmlp_ln_g, gmlp_ln_b, gmlp_w_s, gmlp_b_s, gmlp_w_out, w_o, norm2_g, ffn_w_up, ffn_conv_w, ffn_conv_b, ffn_w_down, final_norm_g):
    p = dict(norm1_g=norm1_g, w_in=w_in, w_gate=w_gate, b_gate=b_gate, conf_conv_w=conf_conv_w,
             conf_conv_b=conf_conv_b, conf_ln_g=conf_ln_g, conf_ln_b=conf_ln_b, conf_w_out=conf_w_out,
             ssm_conv_w=ssm_conv_w, ssm_conv_b=ssm_conv_b, ssm_dt_bias=ssm_dt_bias, ssm_a_log=ssm_a_log,
             ssm_d=ssm_d, ssm_norm_g=ssm_norm_g, ssm_w_out=ssm_w_out, gmlp_ln_g=gmlp_ln_g, gmlp_ln_b=gmlp_ln_b,
             gmlp_w_s=gmlp_w_s, gmlp_b_s=gmlp_b_s, gmlp_w_out=gmlp_w_out, w_o=w_o, norm2_g=norm2_g,
             ffn_w_up=ffn_w_up, ffn_conv_w=ffn_conv_w, ffn_conv_b=ffn_conv_b, ffn_w_down=ffn_w_down,
             final_norm_g=final_norm_g)
    depth = w_in.shape[0]
    d = x_prompt.shape[-1]
    ns, ts, _ = x_sample.shape
    heads, headdim, dstate = state_ssm.shape[2:]
    conv_dim = state_conv_ssm.shape[-1]
    dinner = heads * headdim
    dims = Dims(d=d, cd=state_conv_a.shape[-1], kw_a=conf_conv_w.shape[1], dinner=dinner, heads=heads,
                headdim=headdim, groups=(conv_dim - dinner) // (2 * dstate), dstate=dstate,
                kw_s=ssm_conv_w.shape[1], conv_dim=conv_dim, gd=gmlp_ln_g.shape[-1], ggroups=gmlp_w_s.shape[1],
                dff=state_conv_ffn.shape[-1], kw_f=ffn_conv_w.shape[1])
    assert 2 * headdim == LANES and dstate == LANES and heads <= LANES and heads % (2 * dims.groups) == 0
    assert dims.gd // dims.ggroups == CHUNK == gmlp_w_s.shape[-1]
    assert ts <= SUBLANES and dims.kw_a - 1 <= HIST_A_ROWS and max(dims.kw_s, dims.kw_f) - 1 <= HIST_S_ROWS
    tc = min(PROMPT_TILE, x_prompt.shape[1])
    nb1 = min(SAMPLE_MIX_BLOCK, ns)
    nb2 = min(STATE_BLOCK, ns)
    nb3 = min(SAMPLE_FFN_BLOCK, ns)

    xp = x_prompt
    xs_tm = jnp.transpose(x_sample, (1, 0, 2))
    ha_tm = jnp.transpose(state_conv_a, (0, 2, 1, 3))
    hs_tm = jnp.transpose(state_conv_ssm, (0, 2, 1, 3))
    hf_tm = jnp.transpose(state_conv_ffn, (0, 2, 1, 3))
    h0_all = state_ssm.reshape(depth, ns, dinner, dstate)
    to_sm = lambda a: jnp.transpose(a, (1, 0, 2))
    pa, ph, pc, pf, sa, sh, sc, sf, sv = ([] for _ in range(9))
    for l in range(depth):
        lw = _layer_weights(l, dims, ts, p)
        final = l == depth - 1
        xp, hista, hssm, hists = _prompt_mixer(xp, lw, dims, tc)
        xp, histf = _prompt_ffn(xp, lw, dims, tc, final)
        pa.append(hista[:, HIST_A_ROWS - (dims.kw_a - 1):])
        ph.append(hssm.reshape(-1, heads, headdim, dstate))
        pc.append(hists[:, HIST_S_ROWS - (dims.kw_s - 1):])
        pf.append(histf[:, HIST_S_ROWS - (dims.kw_f - 1):])
        (mixed, ypart, eexp, z, gate1, xw, bmat, cmat, dec, nha, nhs, vn) = _sample_s1(
            xs_tm, ha_tm[l], hs_tm[l], lw, dims, nb1)
        hn, yoff = _sample_s2(dec[:, :heads], h0_all[l], to_sm(xw), to_sm(bmat), to_sm(cmat), dims, nb2)
        xs_tm, nhf = _sample_s3(xs_tm, mixed, ypart, eexp, to_sm(yoff)[:ts], z, gate1, hf_tm[l], lw, dims, nb3, final)
        sa.append(to_sm(nha))
        sh.append(hn.reshape(ns, heads, headdim, dstate))
        sc.append(to_sm(nhs))
        sf.append(to_sm(nhf))
        sv.append(to_sm(vn))
    st = jnp.stack
    return (xp, to_sm(xs_tm), st(pa), st(ph), st(pc), st(pf), st(sa), st(sh), st(sc), st(sf), st(sv))
```

```python
import collections
import functools
import math

import jax
import jax.numpy as jnp
from jax import lax
from jax.experimental import pallas as pl
from jax.experimental.pallas import tpu as pltpu

F32 = jnp.float32
BF16 = jnp.bfloat16
EPS = 1e-6
CHUNK = 128
LANES = 128
SUBLANES = 8
VMEM_LIMIT_BYTES = 56 * 1024 * 1024
PROMPT_TILE = 256
SAMPLE_MIX_BLOCK = 32
SAMPLE_FFN_BLOCK = 64
STATE_BLOCK = 8
FFN_COL_BLOCK = 1024

Dims = collections.namedtuple(
    "Dims", "d cd kw_a dinner heads headdim groups dstate kw_s conv_dim gd ggroups dff kw_f")


def _sigmoid(x):
    return 1.0 / (1.0 + jnp.exp(-x))


def _silu(x):
    return x * _sigmoid(x)


def _gelu(x):
    return 0.5 * x * (1.0 + jnp.tanh(math.sqrt(2.0 / math.pi) * (x + 0.044715 * (x * x * x))))


def _softplus(x):
    return jnp.maximum(x, 0.0) + jnp.log1p(jnp.exp(-jnp.abs(x)))


def _rmsnorm(x, g):
    return x * lax.rsqrt(jnp.mean(x * x, axis=-1, keepdims=True) + EPS) * g


def _layernorm(x, g, b):
    xc = x - jnp.mean(x, axis=-1, keepdims=True)
    var = jnp.mean(xc * xc, axis=-1, keepdims=True)
    return xc * lax.rsqrt(var + EPS) * g + b


def _mm(a, b):
    return jnp.dot(a, b, preferred_element_type=F32)


def _mm_nt(a, b):
    return lax.dot_general(a, b, (((1,), (1,)), ((), ())), preferred_element_type=F32)


def _mm_tn(a, b):
    return lax.dot_general(a, b, (((0,), (0,)), ((), ())), preferred_element_type=F32)


def _split_bf16(v):
    hi = v.astype(BF16)
    lo = (v - hi.astype(F32)).astype(BF16)
    return jnp.concatenate([hi, lo], axis=1)


def _expand(v, sel2):
    return _mm(_split_bf16(v), sel2)


def _conv_taps(w_ref, b_ref, kw, tap):
    acc = b_ref[...] + w_ref[0:1, :] * tap(0)
    for k in range(1, kw):
        acc = acc + w_ref[k:k + 1, :] * tap(k)
    return acc


def _conv_rows(w_ref, b_ref, kw, scr, base, tc):
    acc = b_ref[...]
    for r in range(SUBLANES):
        taps = [k for k in range(kw) if (base + k) % SUBLANES == r]
        if not taps:
            continue
        span = tc + ((base + taps[-1]) // SUBLANES) * SUBLANES
        shifted = scr[pl.ds(r, span), :]
        part = None
        for k in taps:
            q = (base + k) // SUBLANES
            term = w_ref[k:k + 1, :] * shifted[q * SUBLANES:q * SUBLANES + tc]
            part = term if part is None else part + term
        acc = acc + part
    return acc


def _iota2(shape, dim):
    return lax.broadcasted_iota(jnp.int32, shape, dim)


HIST_A_ROWS = 32
HIST_S_ROWS = 8


def _mixer_kernel(x_ref, g1_ref, win_ref, wgate_ref, bgate_ref,
                  cw_ref, cb_ref, clg_ref, clb_ref, cwo_ref,
                  scw_ref, scb_ref, dtb_ref, alog_ref, dskip_ref, sng_ref, swo_ref,
                  glg_ref, glb_ref, gws_ref, gbst_ref, gwo_ref, wo_ref, e2_ref,
                  xo_ref, hista_ref, hssm_ref, hists_ref,
                  hb_scr, xea_scr, xes_scr, st_scr, y_scr, su_scr, *, dims, tc):
    dm = dims
    t = pl.program_id(1)
    last_t = pl.num_programs(1) - 1
    L = CHUNK
    S = dm.dstate
    pair_w = 2 * dm.headdim
    pairs_per_group = dm.heads // dm.groups // 2
    o_z = 2 * dm.cd
    o_xbc = o_z + dm.dinner
    o_u = o_xbc + dm.conv_dim
    o_v = o_u + dm.gd
    o_dt = o_v + dm.gd

    @pl.when(t == 0)
    def _():
        xea_scr[0:HIST_A_ROWS, :] = jnp.zeros((HIST_A_ROWS, dm.cd), F32)
        xes_scr[0:HIST_S_ROWS, :] = jnp.zeros((HIST_S_ROWS, dm.conv_dim), F32)
        st_scr[...] = jnp.zeros(st_scr.shape, F32)

    x = x_ref[...]
    hb_scr[...] = _rmsnorm(x, g1_ref[...]).astype(BF16)

    def gate_pre(i):
        return _mm(hb_scr[...], wgate_ref[:, i * dm.d:(i + 1) * dm.d]) + bgate_ref[:, i * dm.d:(i + 1) * dm.d]

    a12 = _mm(hb_scr[...], win_ref[:, 0:2 * dm.cd])
    xea_scr[HIST_A_ROWS:HIST_A_ROWS + tc, :] = a12[:, :dm.cd] * _sigmoid(a12[:, dm.cd:])
    xes_scr[HIST_S_ROWS:HIST_S_ROWS + tc, :] = _mm(hb_scr[...], win_ref[:, o_xbc:o_xbc + dm.conv_dim])
    dt = _softplus(_mm(hb_scr[...], win_ref[:, o_dt:o_dt + LANES]) + dtb_ref[...])
    uv = _mm(hb_scr[...], win_ref[:, o_u:o_u + 2 * dm.gd])
    z = _mm(hb_scr[...], win_ref[:, o_z:o_z + dm.dinner])
    gate_a, gate_b, gate_c = gate_pre(0), gate_pre(1), gate_pre(2)

    base_a = HIST_A_ROWS - (dm.kw_a - 1)
    ca = _conv_rows(cw_ref, cb_ref, dm.kw_a, xea_scr, base_a, tc)
    xea_scr[0:HIST_A_ROWS, :] = xea_scr[tc:tc + HIST_A_ROWS, :]
    ya = _mm(_silu(_layernorm(ca, clg_ref[...], clb_ref[...])).astype(BF16), cwo_ref[...])
    mixed = _sigmoid(gate_a) * ya

    base_s = HIST_S_ROWS - (dm.kw_s - 1)
    xc = _silu(_conv_taps(scw_ref, scb_ref, dm.kw_s, lambda k: xes_scr[pl.ds(base_s + k, tc), :]))
    xes_scr[0:HIST_S_ROWS, :] = xes_scr[tc:tc + HIST_S_ROWS, :]

    a_neg = jnp.where(_iota2((1, LANES), 1) < dm.heads, -jnp.exp(alog_ref[...]), 0.0)
    causal = _iota2((L, L), 0) >= _iota2((L, L), 1)
    tril_f32 = jnp.where(causal, 1.0, 0.0)
    lo = _iota2((1, pair_w), 1) < dm.headdim
    e2 = e2_ref[...]

    gw = dm.dinner // dm.groups
    for c in range(tc // L):
        r0 = c * L
        dtc = dt[r0:r0 + L]
        acs = jnp.dot(tril_f32, dtc * a_neg, precision=lax.Precision.HIGHEST, preferred_element_type=F32)
        acs_t = acs.T
        e_exp = _expand(jnp.exp(acs), e2)
        w_exp = _expand(jnp.exp(acs[L - 1:L, :] - acs) * dtc, e2)
        dt_exp = _expand(dtc, e2)
        decrow = e_exp[L - 1:L, :]
        xs_c = xc[r0:r0 + L, :dm.dinner]
        xdt = xs_c * dt_exp
        xw = (xs_c * w_exp).astype(BF16)
        for g in range(dm.groups):
            gcols = slice(g * gw, (g + 1) * gw)
            bg = xc[r0:r0 + L, dm.dinner + g * S:dm.dinner + (g + 1) * S].astype(BF16)
            cg = xc[r0:r0 + L, dm.dinner + (dm.groups + g) * S:dm.dinner + (dm.groups + g + 1) * S].astype(BF16)
            cbm = _mm_nt(cg, bg)
            st_g = st_scr[:, gcols]
            y_off = _mm(cg, st_g.astype(BF16)) * e_exp[:, gcols]
            st_scr[:, gcols] = st_g * decrow[:, gcols] + _mm_tn(bg, xw[:, gcols])
            for p in range(pairs_per_group):
                q = g * pairs_per_group + p
                cols = slice(q * pair_w, (q + 1) * pair_w)

                def mix(hh):
                    diff = acs[:, hh:hh + 1] - acs_t[hh:hh + 1, :]
                    return jnp.exp(jnp.where(causal, diff, -jnp.inf)) * cbm

                xq = xdt[:, cols]
                lhs = jnp.concatenate([mix(2 * q), mix(2 * q + 1)], axis=1).astype(BF16)
                rhs = jnp.concatenate([jnp.where(lo, xq, 0.0), jnp.where(lo, 0.0, xq)], axis=0).astype(BF16)
                y_scr[r0:r0 + L, cols] = (_mm(lhs, rhs) + y_off[:, p * pair_w:(p + 1) * pair_w]
                                          + dskip_ref[:, cols] * xs_c[:, cols])

    yb = _mm(_rmsnorm(y_scr[...] * _silu(z), sng_ref[...]).astype(BF16), swo_ref[...])
    mixed = mixed + _sigmoid(gate_b) * yb

    u = _gelu(uv[:, :dm.gd])
    vn = _layernorm(_gelu(uv[:, dm.gd:]), glg_ref[...], glb_ref[...])
    cgw = dm.gd // dm.ggroups
    for g in range(dm.ggroups):
        wt = jnp.where(causal, gws_ref[g], 0.0).astype(BF16)
        for c in range(tc // L):
            r0 = c * L
            s = _mm(wt, vn[r0:r0 + L, g * cgw:(g + 1) * cgw].astype(BF16)) + gbst_ref[:, g:g + 1]
            su_scr[r0:r0 + L, g * cgw:(g + 1) * cgw] = (u[r0:r0 + L, g * cgw:(g + 1) * cgw] * s).astype(BF16)
    yc = _mm(su_scr[...], gwo_ref[...])
    mixed = mixed + _sigmoid(gate_c) * yc

    xo_ref[...] = x + _mm(mixed.astype(BF16), wo_ref[...])

    @pl.when(t == last_t)
    def _():
        hista_ref[...] = xea_scr[0:HIST_A_ROWS, :]
        hists_ref[...] = xes_scr[0:HIST_S_ROWS, :]
        hssm_ref[...] = st_scr[...].T


def _const_spec(arr):
    nd = arr.ndim
    return pl.BlockSpec(arr.shape, lambda *_: (0,) * nd, pipeline_mode=pl.Buffered(1))


def _prompt_mixer(x, lw, dims, tc):
    nb, seq, d = x.shape
    assert seq % tc == 0 and tc % CHUNK == 0
    dm = dims
    consts = [lw[k] for k in ("g1", "win", "wgate", "bgate", "cw", "cb", "clg", "clb", "cwo",
                              "scw", "scb", "dtb", "alog", "dskip", "sng", "swo",
                              "glg", "glb", "gws", "gbst", "gwo", "wo", "e2")]
    hd = dm.heads * dm.headdim
    out_shape = (jax.ShapeDtypeStruct((nb, seq, d), F32),
                 jax.ShapeDtypeStruct((nb, HIST_A_ROWS, dm.cd), F32),
                 jax.ShapeDtypeStruct((nb, hd, dm.dstate), F32),
                 jax.ShapeDtypeStruct((nb, HIST_S_ROWS, dm.conv_dim), F32))
    return pl.pallas_call(
        functools.partial(_mixer_kernel, dims=dm, tc=tc),
        grid=(nb, seq // tc),
        in_specs=[pl.BlockSpec((None, tc, d), lambda b, t: (b, t, 0))] + [_const_spec(c) for c in consts],
        out_specs=(pl.BlockSpec((None, tc, d), lambda b, t: (b, t, 0)),
                   pl.BlockSpec((None, HIST_A_ROWS, dm.cd), lambda b, t: (b, 0, 0)),
                   pl.BlockSpec((None, hd, dm.dstate), lambda b, t: (b, 0, 0)),
                   pl.BlockSpec((None, HIST_S_ROWS, dm.conv_dim), lambda b, t: (b, 0, 0))),
        out_shape=out_shape,
        scratch_shapes=[pltpu.VMEM((tc, d), BF16),
                        pltpu.VMEM((HIST_A_ROWS + tc, dm.cd), F32),
                        pltpu.VMEM((HIST_S_ROWS + tc, dm.conv_dim), F32),
                        pltpu.VMEM((dm.dstate, hd), F32),
                        pltpu.VMEM((tc, hd), F32),
                        pltpu.VMEM((tc, dm.gd), BF16)],
        compiler_params=pltpu.CompilerParams(dimension_semantics=("arbitrary", "arbitrary"),
                                             vmem_limit_bytes=VMEM_LIMIT_BYTES),
        name="prompt_mixer",
    )(x, *consts)


def _ffn_kernel(x_ref, g2_ref, wup_ref, fcw_ref, fcb_ref, wdown_ref, fng_ref,
                xo_ref, histf_ref, hb_scr, xef_scr, act_scr, *, dims, tc, final_norm, col_blocks):
    dm = dims
    t = pl.program_id(1)
    last_t = pl.num_programs(1) - 1

    @pl.when(t == 0)
    def _():
        xef_scr[0:HIST_S_ROWS, :] = jnp.zeros((HIST_S_ROWS, dm.dff), F32)

    x = x_ref[...]
    hb_scr[...] = _rmsnorm(x, g2_ref[...]).astype(BF16)
    base = HIST_S_ROWS - (dm.kw_f - 1)
    for c0, c1 in col_blocks:
        cols = slice(c0, c1)
        xef_scr[HIST_S_ROWS:HIST_S_ROWS + tc, cols] = _mm(hb_scr[...], wup_ref[:, c0:c1])
        agc = fcb_ref[:, cols] + fcw_ref[0:1, cols] * xef_scr[pl.ds(base, tc), cols]
        for k in range(1, dm.kw_f):
            agc = agc + fcw_ref[k:k + 1, cols] * xef_scr[pl.ds(base + k, tc), cols]
        av = _mm(hb_scr[...], wup_ref[:, dm.dff + c0:dm.dff + c1])
        act_scr[:, cols] = (_silu(agc) * av).astype(BF16)
    xef_scr[0:HIST_S_ROWS, :] = xef_scr[tc:tc + HIST_S_ROWS, :]
    out = x
    for c0, c1 in col_blocks:
        out = out + _mm(act_scr[:, c0:c1], wdown_ref[c0:c1, :])
    if final_norm:
        out = _rmsnorm(out, fng_ref[...])
    xo_ref[...] = out

    @pl.when(t == last_t)
    def _():
        histf_ref[...] = xef_scr[0:HIST_S_ROWS, :]


def _prompt_ffn(x, lw, dims, tc, final_norm):
    nb, seq, d = x.shape
    dm = dims
    consts = [lw[k] for k in ("g2", "wup", "fcw", "fcb", "wdown", "fng")]
    edges = list(range(0, dm.dff, FFN_COL_BLOCK)) + [dm.dff]
    col_blocks = tuple(zip(edges[:-1], edges[1:]))
    return pl.pallas_call(
        functools.partial(_ffn_kernel, dims=dm, tc=tc, final_norm=final_norm, col_blocks=col_blocks),
        grid=(nb, seq // tc),
        in_specs=[pl.BlockSpec((None, tc, d), lambda b, t: (b, t, 0))] + [_const_spec(c) for c in consts],
        out_specs=(pl.BlockSpec((None, tc, d), lambda b, t: (b, t, 0)),
                   pl.BlockSpec((None, HIST_S_ROWS, dm.dff), lambda b, t: (b, 0, 0))),
        out_shape=(jax.ShapeDtypeStruct((nb, seq, d), F32),
                   jax.ShapeDtypeStruct((nb, HIST_S_ROWS, dm.dff), F32)),
        scratch_shapes=[pltpu.VMEM((tc, d), BF16),
                        pltpu.VMEM((HIST_S_ROWS + tc, dm.dff), F32),
                        pltpu.VMEM((tc, dm.dff), BF16)],
        compiler_params=pltpu.CompilerParams(dimension_semantics=("arbitrary", "arbitrary"),
                                             vmem_limit_bytes=VMEM_LIMIT_BYTES),
        name="prompt_ffn",
    )(x, *consts)


def _s1_kernel(x_ref, ha_ref, hs_ref, g1_ref, win_ref, wgate_ref, bgate_ref,
               cw_ref, cb_ref, clg_ref, clb_ref, cwo_ref,
               scw_ref, scb_ref, dtb_ref, alog_ref, dskip_ref,
               glg_ref, glb_ref, gwst_ref, gbst_ref, gwo_ref, e2_ref, r2_ref,
               mixed_ref, ypart_ref, eexp_ref, z_ref, gate1_ref, xw_ref, b_ref, c_ref, dec_ref,
               nha_ref, nhs_ref, vn_ref, hb_scr, *, dims, ts, nb):
    dm = dims
    S2 = dm.groups * dm.dstate
    o_z = 2 * dm.cd
    o_xbc = o_z + dm.dinner
    o_u = o_xbc + dm.conv_dim
    o_v = o_u + dm.gd
    o_dt = o_v + dm.gd
    rows = lambda a, i: a[i * nb:(i + 1) * nb]

    for i in range(ts):
        hb_scr[i * nb:(i + 1) * nb, :] = _rmsnorm(x_ref[i], g1_ref[...]).astype(BF16)
    hb = hb_scr[...]

    a12 = _mm(hb, win_ref[:, 0:2 * dm.cd])
    ga = a12[:, :dm.cd] * _sigmoid(a12[:, dm.cd:])
    na = dm.kw_a - 1
    xe_a = lambda j: ha_ref[j] if j < na else rows(ga, j - na)
    ca = jnp.concatenate([_conv_taps(cw_ref, cb_ref, dm.kw_a, lambda k, i=i: xe_a(i + k)) for i in range(ts)], axis=0)
    for j in range(na):
        nha_ref[j] = xe_a(j + ts)
    ya = _mm(_silu(_layernorm(ca, clg_ref[...], clb_ref[...])).astype(BF16), cwo_ref[...])
    mixed = _sigmoid(_mm(hb, wgate_ref[:, 0:dm.d]) + bgate_ref[:, 0:dm.d]) * ya

    xbc = _mm(hb, win_ref[:, o_xbc:o_xbc + dm.conv_dim])
    ns = dm.kw_s - 1
    xe_s = lambda j: hs_ref[j] if j < ns else rows(xbc, j - ns)
    xc = [_silu(_conv_taps(scw_ref, scb_ref, dm.kw_s, lambda k, i=i: xe_s(i + k))) for i in range(ts)]
    for j in range(ns):
        nhs_ref[j] = xe_s(j + ts)
    xs = [v[:, :dm.dinner] for v in xc]
    bm = [v[:, dm.dinner:dm.dinner + S2] for v in xc]
    cm = [v[:, dm.dinner + S2:dm.dinner + 2 * S2] for v in xc]

    z = _mm(hb, win_ref[:, o_z:o_z + dm.dinner])
    gate1 = _sigmoid(_mm(hb, wgate_ref[:, dm.d:2 * dm.d]) + bgate_ref[:, dm.d:2 * dm.d])
    dt_all = _softplus(_mm(hb, win_ref[:, o_dt:o_dt + LANES]) + dtb_ref[...])
    a_neg = jnp.where(_iota2((1, LANES), 1) < dm.heads, -jnp.exp(alog_ref[...]), 0.0)
    dt = [rows(dt_all, i) for i in range(ts)]
    acs = []
    for i in range(ts):
        a_i = dt[i] * a_neg
        acs.append(a_i if i == 0 else acs[-1] + a_i)
    e2 = e2_ref[...]
    r2 = r2_ref[...]
    for i in range(ts):
        yd = dskip_ref[...] * xs[i]
        for j in range(i + 1):
            cb16 = _expand(cm[i] * bm[j], r2)
            coef = cb16 * jnp.exp(acs[i] - acs[j]) * dt[j]
            yd = yd + _expand(coef, e2) * xs[j]
        ypart_ref[i] = yd
        eexp_ref[i] = _expand(jnp.exp(acs[i]), e2)
        xw_ref[i] = _expand(jnp.exp(acs[ts - 1] - acs[i]) * dt[i], e2) * xs[i]
        b_ref[i] = bm[i]
        c_ref[i] = cm[i]
        z_ref[i] = rows(z, i)
        gate1_ref[i] = rows(gate1, i)
    for i in range(ts, xw_ref.shape[0]):
        xw_ref[i] = jnp.zeros((nb, dm.dinner), F32)
        b_ref[i] = jnp.zeros((nb, S2), F32)
        c_ref[i] = jnp.zeros((nb, S2), F32)
    dec_ref[...] = jnp.exp(acs[ts - 1])

    uv = _mm(hb, win_ref[:, o_u:o_u + 2 * dm.gd])
    u = _gelu(uv[:, :dm.gd])
    vn_all = _layernorm(_gelu(uv[:, dm.gd:]), glg_ref[...], glb_ref[...])
    vn = [rows(vn_all, i) for i in range(ts)]
    su = []
    for i in range(ts):
        vn_ref[i] = vn[i]
        s = gbst_ref[i:i + 1, :]
        for j in range(i + 1):
            s = s + gwst_ref[i * ts + j:i * ts + j + 1, :] * vn[j]
        su.append(rows(u, i) * s)
    yc = _mm(jnp.concatenate(su, axis=0).astype(BF16), gwo_ref[...])
    mixed = mixed + _sigmoid(_mm(hb, wgate_ref[:, 2 * dm.d:3 * dm.d]) + bgate_ref[:, 2 * dm.d:3 * dm.d]) * yc
    for i in range(ts):
        mixed_ref[i] = rows(mixed, i)


def _tm_spec(lead, nb, width):
    return pl.BlockSpec((lead, nb, width), lambda i: (0, i, 0))


def _tm_layer_spec(layer, lead, nb, width):
    return pl.BlockSpec((None, lead, nb, width), lambda i: (layer, 0, i, 0))


def _sample_s1(x_tm, ha_tm, hs_tm, layer, lw, dims, nb):
    ts, ns, d = x_tm.shape
    dm = dims
    S2 = dm.groups * dm.dstate
    consts = [lw[k] for k in ("g1", "win", "wgate", "bgate", "cw", "cb", "clg", "clb", "cwo",
                              "scw", "scb", "dtb", "alog", "dskip", "glg", "glb", "gwst", "gbss", "gwo", "e2", "r2")]
    f = lambda lead, w: jax.ShapeDtypeStruct((lead, ns, w), F32)
    out_shape = (f(ts, d), f(ts, dm.dinner), f(ts, dm.dinner), f(ts, dm.dinner), f(ts, d),
                 f(SUBLANES, dm.dinner), f(SUBLANES, S2), f(SUBLANES, S2),
                 jax.ShapeDtypeStruct((ns, LANES), F32),
                 f(dm.kw_a - 1, dm.cd), f(dm.kw_s - 1, dm.conv_dim), f(ts, dm.gd))
    out_specs = (_tm_spec(ts, nb, d), _tm_spec(ts, nb, dm.dinner), _tm_spec(ts, nb, dm.dinner),
                 _tm_spec(ts, nb, dm.dinner), _tm_spec(ts, nb, d),
                 _tm_spec(SUBLANES, nb, dm.dinner), _tm_spec(SUBLANES, nb, S2), _tm_spec(SUBLANES, nb, S2),
                 pl.BlockSpec((nb, LANES), lambda i: (i, 0)),
                 _tm_spec(dm.kw_a - 1, nb, dm.cd), _tm_spec(dm.kw_s - 1, nb, dm.conv_dim), _tm_spec(ts, nb, dm.gd))
    return pl.pallas_call(
        functools.partial(_s1_kernel, dims=dm, ts=ts, nb=nb),
        grid=(ns // nb,),
        in_specs=[_tm_spec(ts, nb, d), _tm_layer_spec(layer, dm.kw_a - 1, nb, dm.cd),
                  _tm_layer_spec(layer, dm.kw_s - 1, nb, dm.conv_dim)] + [_const_spec(c) for c in consts],
        out_specs=out_specs,
        out_shape=out_shape,
        scratch_shapes=[pltpu.VMEM((ts * nb, d), BF16)],
        compiler_params=pltpu.CompilerParams(dimension_semantics=("arbitrary",),
                                             vmem_limit_bytes=VMEM_LIMIT_BYTES),
        name="sample_mixer",
    )(x_tm, ha_tm, hs_tm, *consts)


def _s2_kernel(dec_ref, h0_ref, xw_ref, b_ref, c_ref, *rest, dims, nb):
    hn_ref, yo_ref = rest[-2:]
    dm = dims
    S = dm.dstate
    gw = dm.dinner // dm.groups
    heads_per_group = dm.heads // dm.groups
    i = pl.program_id(0)
    for j in range(nb):
        for g in range(dm.groups):
            r0 = g * gw
            h0g = h0_ref[j, r0:r0 + gw, :]
            cg = c_ref[j, :, g * S:(g + 1) * S]
            yo_ref[j, :, r0:r0 + gw] = _mm_nt(cg.astype(BF16), h0g.astype(BF16))
            upd = _mm_tn(xw_ref[j, :, r0:r0 + gw].astype(BF16), b_ref[j, :, g * S:(g + 1) * S].astype(BF16))
            for hh in range(heads_per_group):
                rr = slice(r0 + hh * dm.headdim, r0 + (hh + 1) * dm.headdim)
                dec = dec_ref[i * nb + j, g * heads_per_group + hh]
                hn_ref[j, rr, :] = dec * h0_ref[j, rr, :] + upd[hh * dm.headdim:(hh + 1) * dm.headdim]


def _sample_s2(dec, h0_all, hn_all, layer, xw_sm, b_sm, c_sm, dims, nb):
    depth, ns, hd, S = h0_all.shape
    dm = dims
    S2 = dm.groups * dm.dstate
    blk = lambda r, w: pl.BlockSpec((nb, r, w), lambda i: (i, 0, 0))
    layer_blk = pl.BlockSpec((None, nb, hd, S), lambda i: (layer, i, 0, 0))
    in_specs = [pl.BlockSpec(memory_space=pltpu.SMEM), layer_blk, blk(SUBLANES, hd), blk(SUBLANES, S2), blk(SUBLANES, S2)]
    args = [dec, h0_all, xw_sm, b_sm, c_sm]
    aliases = {}
    if hn_all is not None:
        aliases = {len(args): 0}
        in_specs.append(pl.BlockSpec(memory_space=pl.ANY))
        args.append(hn_all)
    return pl.pallas_call(
        functools.partial(_s2_kernel, dims=dm, nb=nb),
        grid=(ns // nb,),
        in_specs=in_specs,
        out_specs=(layer_blk, blk(SUBLANES, hd)),
        out_shape=(jax.ShapeDtypeStruct((depth, ns, hd, S), F32), jax.ShapeDtypeStruct((ns, SUBLANES, hd), F32)),
        input_output_aliases=aliases,
        compiler_params=pltpu.CompilerParams(dimension_semantics=("arbitrary",),
                                             vmem_limit_bytes=VMEM_LIMIT_BYTES),
        name="sample_state",
    )(*args)


def _s3_kernel(x_ref, mixed_ref, ypart_ref, eexp_ref, yoff_ref, z_ref, gate1_ref, hf_ref,
               sng_ref, swo_ref, wo_ref, g2_ref, wup_ref, fcw_ref, fcb_ref, wdown_ref, fng_ref,
               xo_ref, nhf_ref, *, dims, ts, nb, final_norm):
    dm = dims
    cat = lambda ref: jnp.concatenate([ref[i] for i in range(ts)], axis=0)
    rows = lambda a, i: a[i * nb:(i + 1) * nb]
    y = (cat(ypart_ref) + cat(eexp_ref) * cat(yoff_ref)) * _silu(cat(z_ref))
    yb = _mm(_rmsnorm(y, sng_ref[...]).astype(BF16), swo_ref[...])
    mixed = cat(mixed_ref) + cat(gate1_ref) * yb
    x1 = cat(x_ref) + _mm(mixed.astype(BF16), wo_ref[...])

    hb = _rmsnorm(x1, g2_ref[...]).astype(BF16)
    ag = _mm(hb, wup_ref[:, 0:dm.dff])
    av = _mm(hb, wup_ref[:, dm.dff:2 * dm.dff])
    nf = dm.kw_f - 1
    xe = lambda j: hf_ref[j] if j < nf else rows(ag, j - nf)
    agc = jnp.concatenate([_conv_taps(fcw_ref, fcb_ref, dm.kw_f, lambda k, i=i: xe(i + k)) for i in range(ts)], axis=0)
    for j in range(nf):
        nhf_ref[j] = xe(j + ts)
    out = x1 + _mm((_silu(agc) * av).astype(BF16), wdown_ref[...])
    if final_norm:
        out = _rmsnorm(out, fng_ref[...])
    for i in range(ts):
        xo_ref[i] = rows(out, i)


def _sample_s3(x_tm, mixed, ypart, eexp, yoff_tm, z, gate1, hf_tm, layer, lw, dims, nb, final_norm):
    ts, ns, d = x_tm.shape
    dm = dims
    consts = [lw[k] for k in ("sng", "swo", "wo", "g2", "wup", "fcw", "fcb", "wdown", "fng")]
    return pl.pallas_call(
        functools.partial(_s3_kernel, dims=dm, ts=ts, nb=nb, final_norm=final_norm),
        grid=(ns // nb,),
        in_specs=[_tm_spec(ts, nb, d), _tm_spec(ts, nb, d), _tm_spec(ts, nb, dm.dinner), _tm_spec(ts, nb, dm.dinner),
                  _tm_spec(ts, nb, dm.dinner), _tm_spec(ts, nb, dm.dinner), _tm_spec(ts, nb, d),
                  _tm_layer_spec(layer, dm.kw_f - 1, nb, dm.dff)] + [_const_spec(c) for c in consts],
        out_specs=(_tm_spec(ts, nb, d), _tm_spec(dm.kw_f - 1, nb, dm.dff)),
        out_shape=(jax.ShapeDtypeStruct((ts, ns, d), F32), jax.ShapeDtypeStruct((dm.kw_f - 1, ns, dm.dff), F32)),
        compiler_params=pltpu.CompilerParams(dimension_semantics=("arbitrary",),
                                             vmem_limit_bytes=VMEM_LIMIT_BYTES),
        name="sample_out_ffn",
    )(x_tm, mixed, ypart, eexp, yoff_tm, z, gate1, hf_tm, *consts)


def _pad_lanes(v, width=LANES):
    return jnp.pad(v, [(0, 0)] * (v.ndim - 1) + [(0, width - v.shape[-1])])


def _layer_weights(l, dims, ts, p):
    dm = dims
    row = lambda v: v.reshape(1, -1)
    w_in = p["w_in"][l]
    o_dt = 2 * dm.cd + dm.dinner + dm.conv_dim
    win = jnp.concatenate([w_in[:, :o_dt], w_in[:, o_dt + dm.heads:], _pad_lanes(w_in[:, o_dt:o_dt + dm.heads])],
                          axis=1).astype(BF16)
    hd = dm.heads * dm.headdim
    head_of_col = jnp.arange(hd) // dm.headdim
    e1 = (jnp.arange(LANES)[:, None] == head_of_col[None, :])
    group_of_row = jnp.arange(dm.groups * dm.dstate) // dm.dstate
    group_of_head = jnp.arange(LANES) // (dm.heads // dm.groups)
    r1 = (group_of_row[:, None] == group_of_head[None, :]) & (jnp.arange(LANES)[None, :] < dm.heads)
    cgw = dm.gd // dm.ggroups
    gws = p["gmlp_w_s"][l]
    gbs = p["gmlp_b_s"][l]
    return dict(
        g1=row(p["norm1_g"][l]), win=win, wgate=p["w_gate"][l].astype(BF16), bgate=row(p["b_gate"][l]),
        cw=p["conf_conv_w"][l], cb=row(p["conf_conv_b"][l]), clg=row(p["conf_ln_g"][l]), clb=row(p["conf_ln_b"][l]),
        cwo=p["conf_w_out"][l].astype(BF16),
        scw=p["ssm_conv_w"][l], scb=row(p["ssm_conv_b"][l]),
        dtb=_pad_lanes(row(p["ssm_dt_bias"][l])), alog=_pad_lanes(row(p["ssm_a_log"][l])),
        dskip=row(jnp.repeat(p["ssm_d"][l], dm.headdim)),
        sng=row(p["ssm_norm_g"][l]), swo=p["ssm_w_out"][l].astype(BF16),
        glg=row(p["gmlp_ln_g"][l]), glb=row(p["gmlp_ln_b"][l]),
        gws=gws, gbst=gbs.T, gwo=p["gmlp_w_out"][l].astype(BF16), wo=p["w_o"][l].astype(BF16),
        gwst=jnp.repeat(jnp.transpose(gws[:, :ts, :ts], (1, 2, 0)).reshape(ts * ts, dm.ggroups), cgw, axis=1),
        gbss=jnp.repeat(gbs[:, :ts].T, cgw, axis=1),
        e2=jnp.concatenate([e1, e1], axis=0).astype(BF16), r2=jnp.concatenate([r1, r1], axis=0).astype(BF16),
        g2=row(p["norm2_g"][l]), wup=p["ffn_w_up"][l].astype(BF16), fcw=p["ffn_conv_w"][l],
        fcb=row(p["ffn_conv_b"][l]), wdown=p["ffn_w_down"][l].astype(BF16), fng=row(p["final_norm_g"]),
    )


def kernel(x_prompt, x_sample, state_conv_a, state_ssm, state_conv_ssm, state_conv_ffn, norm1_g, w_in, w_gate, b_gate, conf_conv_w, conf_conv_b, conf_ln_g, conf_ln_b, conf_w_out, ssm_conv_w, ssm_conv_b, ssm_dt_bias, ssm_a_log, ssm_d, ssm_norm_g, ssm_w_out, gmlp_ln_g, gmlp_ln_b, gmlp_w_s, gmlp_b_s, gmlp_w_out, w_o, norm2_g, ffn_w_up, ffn_conv_w, ffn_conv_b, ffn_w_down, final_norm_g):
    p = dict(norm1_g=norm1_g, w_in=w_in, w_gate=w_gate, b_gate=b_gate, conf_conv_w=conf_conv_w,
             conf_conv_b=conf_conv_b, conf_ln_g=conf_ln_g, conf_ln_b=conf_ln_b, conf_w_out=conf_w_out,
             ssm_conv_w=ssm_conv_w, ssm_conv_b=ssm_conv_b, ssm_dt_bias=ssm_dt_bias, ssm_a_log=ssm_a_log,
             ssm_d=ssm_d, ssm_norm_g=ssm_norm_g, ssm_w_out=ssm_w_out, gmlp_ln_g=gmlp_ln_g, gmlp_ln_b=gmlp_ln_b,
             gmlp_w_s=gmlp_w_s, gmlp_b_s=gmlp_b_s, gmlp_w_out=gmlp_w_out, w_o=w_o, norm2_g=norm2_g,
             ffn_w_up=ffn_w_up, ffn_conv_w=ffn_conv_w, ffn_conv_b=ffn_conv_b, ffn_w_down=ffn_w_down,
             final_norm_g=final_norm_g)
    depth = w_in.shape[0]
    d = x_prompt.shape[-1]
    ns, ts, _ = x_sample.shape
    heads, headdim, dstate = state_ssm.shape[2:]
    conv_dim = state_conv_ssm.shape[-1]
    dinner = heads * headdim
    dims = Dims(d=d, cd=state_conv_a.shape[-1], kw_a=conf_conv_w.shape[1], dinner=dinner, heads=heads,
                headdim=headdim, groups=(conv_dim - dinner) // (2 * dstate), dstate=dstate,
                kw_s=ssm_conv_w.shape[1], conv_dim=conv_dim, gd=gmlp_ln_g.shape[-1], ggroups=gmlp_w_s.shape[1],
                dff=state_conv_ffn.shape[-1], kw_f=ffn_conv_w.shape[1])
    assert 2 * headdim == LANES and dstate == LANES and heads <= LANES and heads % (2 * dims.groups) == 0
    assert dims.gd // dims.ggroups == CHUNK == gmlp_w_s.shape[-1]
    assert ts <= SUBLANES and dims.kw_a - 1 <= HIST_A_ROWS and max(dims.kw_s, dims.kw_f) - 1 <= HIST_S_ROWS
    tc = min(PROMPT_TILE, x_prompt.shape[1])
    nb1 = min(SAMPLE_MIX_BLOCK, ns)
    nb2 = min(STATE_BLOCK, ns)
    nb3 = min(SAMPLE_FFN_BLOCK, ns)

    xp = x_prompt
    xs_tm = jnp.transpose(x_sample, (1, 0, 2))
    ha_tm = jnp.transpose(state_conv_a, (0, 2, 1, 3))
    hs_tm = jnp.transpose(state_conv_ssm, (0, 2, 1, 3))
    hf_tm = jnp.transpose(state_conv_ffn, (0, 2, 1, 3))
    h0_all = state_ssm.reshape(depth, ns, dinner, dstate)
    to_sm = lambda a: jnp.transpose(a, (1, 0, 2))
    pa, ph, pc, pf, sa, sc, sf, sv = ([] for _ in range(8))
    hn_all = None
    for l in range(depth):
        lw = _layer_weights(l, dims, ts, p)
        final = l == depth - 1
        xp, hista, hssm, hists = _prompt_mixer(xp, lw, dims, tc)
        xp, histf = _prompt_ffn(xp, lw, dims, tc, final)
        pa.append(hista[:, HIST_A_ROWS - (dims.kw_a - 1):])
        ph.append(hssm.reshape(-1, heads, headdim, dstate))
        pc.append(hists[:, HIST_S_ROWS - (dims.kw_s - 1):])
        pf.append(histf[:, HIST_S_ROWS - (dims.kw_f - 1):])
        (mixed, ypart, eexp, z, gate1, xw, bmat, cmat, dec, nha, nhs, vn) = _sample_s1(
            xs_tm, ha_tm, hs_tm, l, lw, dims, nb1)
        hn_all, yoff = _sample_s2(dec[:, :heads], h0_all, hn_all, l, to_sm(xw), to_sm(bmat), to_sm(cmat), dims, nb2)
        xs_tm, nhf = _sample_s3(xs_tm, mixed, ypart, eexp, to_sm(yoff)[:ts], z, gate1, hf_tm, l, lw, dims, nb3, final)
        sa.append(to_sm(nha))
        sc.append(to_sm(nhs))
        sf.append(to_sm(nhf))
        sv.append(to_sm(vn))
    st = jnp.stack
    new_ssm_s = hn_all.reshape(depth, ns, heads, headdim, dstate)
    return (xp, to_sm(xs_tm), st(pa), st(ph), st(pc), st(pf), st(sa), new_ssm_s, st(sc), st(sf), st(sv))
```
